```python
import math, functools
import jax, jax.numpy as jnp
from jax import lax
import numpy as np

D_MODEL = 1024
BATCH = 16
SEQ = 4096
DEPTH = 1
DEC_BATCH = 128
DEC_SEQ = 4
PAST_LEN = 8192
PAGE_SIZE = 128

HEAD_DIM = 64
MOBA_HEADS = (D_MODEL // 2) // HEAD_DIM
MOBA_BLOCK = 256
MOBA_TOPK = 3
Q_CHUNK = 128
ROPE_DIM = HEAD_DIM // 4
ROPE_THETA = 500000.0
MOBA_SCALE = HEAD_DIM ** -0.5
RET_HEADS = 4
RET_V_DIM = (D_MODEL // 2) // RET_HEADS
RET_QK_DIM = RET_V_DIM // 2
RET_CHUNK = 128
RET_THETA = 10000.0
D_FF = ((8 * D_MODEL + 3 * 256 - 1) // (3 * 256)) * 256
NORM_EPS = 1e-6
A_WIDTH = MOBA_HEADS * HEAD_DIM
R_QK_WIDTH = RET_HEADS * RET_QK_DIM
R_V_WIDTH = RET_HEADS * RET_V_DIM
SPLIT_SIZES = (A_WIDTH, A_WIDTH, A_WIDTH, R_QK_WIDTH, R_QK_WIDTH, R_V_WIDTH, R_V_WIDTH)
IN_COLS = sum(SPLIT_SIZES)
MIX_WIDTH = A_WIDTH + R_V_WIDTH

kernel_name = "hymba_moba_retention_decode_step"


def rmsnorm(x, g):
    xf = x.astype(jnp.float32)
    r = xf * lax.rsqrt(jnp.mean(xf * xf, axis=-1, keepdims=True) + NORM_EPS)
    return (r * g.astype(jnp.float32)).astype(x.dtype)


def split_cols(u):
    outs = []
    off = 0
    for s in SPLIT_SIZES:
        outs.append(u[..., off:off + s])
        off += s
    return outs


def rotary(x, pos, rot_dim, theta):
    half = rot_dim // 2
    inv_freq = jnp.power(theta, -jnp.arange(half, dtype=jnp.float32) * (2.0 / rot_dim))
    ang = pos.astype(jnp.float32)[:, None] * inv_freq[None, :]
    cos = jnp.cos(ang)[None, :, None, :]
    sin = jnp.sin(ang)[None, :, None, :]
    xf = x.astype(jnp.float32)
    x1 = xf[..., :half]
    x2 = xf[..., half:rot_dim]
    out = jnp.concatenate([x1 * cos - x2 * sin, x2 * cos + x1 * sin, xf[..., rot_dim:]], axis=-1)
    return out.astype(x.dtype)


def moba_one(q, k, v, pos0):
    lq = q.shape[0]
    lk = k.shape[0]
    n_blk = -(-lk // MOBA_BLOCK)
    pad = n_blk * MOBA_BLOCK - lk

    def to_blocks(t):
        t = jnp.pad(t, ((0, pad), (0, 0), (0, 0)))
        return t.reshape(n_blk, MOBA_BLOCK, MOBA_HEADS, HEAD_DIM).transpose(2, 0, 1, 3)

    kb = to_blocks(k)
    vb = to_blocks(v)
    k_mean = jnp.mean(kb.astype(jnp.float32), axis=2)
    n_sel = min(MOBA_TOPK, n_blk)
    qc = min(Q_CHUNK, lq)
    n_qc = lq // qc
    q_chunks = q.reshape(n_qc, qc, MOBA_HEADS, HEAD_DIM)
    head_idx = jnp.arange(MOBA_HEADS)[:, None, None]
    blk_off = jnp.arange(MOBA_BLOCK, dtype=jnp.int32)

    def attend_chunk(args):
        q_c, c = args
        pos = pos0 + c * qc + jnp.arange(qc, dtype=jnp.int32)
        cur = pos // MOBA_BLOCK
        gate = jnp.einsum('qhd,hjd->hqj', q_c.astype(jnp.float32), k_mean)
        past = jnp.arange(n_blk, dtype=jnp.int32)[None, None, :] < cur[None, :, None]
        gate = jnp.where(past, gate, -jnp.inf)
        _, sel = lax.top_k(gate, n_sel)
        sel = sel.astype(jnp.int32)
        sel_ok = sel < cur[None, :, None]
        own = jnp.broadcast_to(cur[None, :, None], (MOBA_HEADS, qc, 1))
        blocks = jnp.concatenate([sel, own], axis=-1)
        kg = kb[head_idx, blocks]
        vg = vb[head_idx, blocks]
        s = jnp.einsum('qhd,hqjkd->hqjk', q_c, kg, preferred_element_type=jnp.float32) * MOBA_SCALE
        own_ok = (cur[:, None] * MOBA_BLOCK + blk_off[None, :]) <= pos[:, None]
        ok = jnp.concatenate([
            jnp.broadcast_to(sel_ok[..., None], (MOBA_HEADS, qc, n_sel, MOBA_BLOCK)),
            jnp.broadcast_to(own_ok[None, :, None, :], (MOBA_HEADS, qc, 1, MOBA_BLOCK))], axis=2)
        s = jnp.where(ok, s, -jnp.inf).reshape(MOBA_HEADS, qc, (n_sel + 1) * MOBA_BLOCK)
        p = jax.nn.softmax(s, axis=-1).reshape(MOBA_HEADS, qc, n_sel + 1, MOBA_BLOCK)
        o = jnp.einsum('hqjk,hqjkd->qhd', p.astype(vg.dtype), vg, preferred_element_type=jnp.float32)
        return o.astype(q.dtype)

    out = lax.map(attend_chunk, (q_chunks, jnp.arange(n_qc, dtype=jnp.int32)))
    return out.reshape(lq, MOBA_HEADS, HEAD_DIM)


def moba_prompt(q, k, v):
    return lax.map(lambda a: moba_one(a[0], a[1], a[2], 0), (q, k, v))


def moba_sample(cache_k_l, cache_v_l, page_table, q, k, v):
    past_len = page_table.shape[1] * cache_k_l.shape[1]

    def one(a):
        qb, kb, vb, pt = a
        k_past = cache_k_l[pt].reshape(past_len, MOBA_HEADS, HEAD_DIM)
        v_past = cache_v_l[pt].reshape(past_len, MOBA_HEADS, HEAD_DIM)
        k_all = jnp.concatenate([k_past, kb.astype(k_past.dtype)], axis=0)
        v_all = jnp.concatenate([v_past, vb.astype(v_past.dtype)], axis=0)
        return moba_one(qb, k_all, v_all, past_len)

    return lax.map(one, (q, k, v, page_table))


def retention(q, k, v, state0):
    b, l, hr, _ = q.shape
    c = min(RET_CHUNK, l)
    n = l // c

    def chunks(t):
        return t.astype(jnp.float32).reshape(b, n, c, hr, t.shape[-1]).transpose(1, 0, 3, 2, 4)

    log_g = jnp.log1p(-jnp.power(2.0, -5.0 - jnp.arange(hr, dtype=jnp.float32)))
    idx = jnp.arange(c, dtype=jnp.float32)
    diff = idx[:, None] - idx[None, :]
    causal = diff >= 0
    decay_mask = jnp.exp(jnp.where(causal, diff, 0.0)[None] * log_g[:, None, None]) * causal[None]
    q_decay = jnp.exp((idx + 1.0)[None, :] * log_g[:, None])
    k_decay = jnp.exp((c - 1.0 - idx)[None, :] * log_g[:, None])
    chunk_decay = jnp.exp(c * log_g)

    def step(s, inp):
        qc, kc, vc = inp
        inner = jnp.einsum('bhnd,bhmd->bhnm', qc, kc) * decay_mask[None]
        o = jnp.einsum('bhnm,bhmv->bhnv', inner, vc) + jnp.einsum(
            'bhnd,bhdv->bhnv', qc * q_decay[None, :, :, None], s)
        s = s * chunk_decay[None, :, None, None] + jnp.einsum(
            'bhmd,bhmv->bhdv', kc * k_decay[None, :, :, None], vc)
        return s, o

    s_fin, o = lax.scan(step, state0.astype(jnp.float32), (chunks(q), chunks(k), chunks(v)))
    o = o.transpose(1, 0, 3, 2, 4).reshape(b, l, hr, v.shape[-1])
    return o, s_fin


def layer(x, pos0, attend, ret_state0, w_in, w_out, g_mix, g_ffn, w_gate, w_up, w_down):
    b, l, _ = x.shape
    pos = pos0 + jnp.arange(l, dtype=jnp.int32)
    h = rmsnorm(x, g_mix)
    u = h @ w_in
    aq, ak, av, rq, rk, rv, rg = split_cols(u)
    aq = rotary(aq.reshape(b, l, MOBA_HEADS, HEAD_DIM), pos, ROPE_DIM, ROPE_THETA)
    ak = rotary(ak.reshape(b, l, MOBA_HEADS, HEAD_DIM), pos, ROPE_DIM, ROPE_THETA)
    av = av.reshape(b, l, MOBA_HEADS, HEAD_DIM)
    a_out = attend(aq, ak, av).reshape(b, l, A_WIDTH)
    rq = rotary(rq.reshape(b, l, RET_HEADS, RET_QK_DIM), pos, RET_QK_DIM, RET_THETA)
    rk = rotary(rk.reshape(b, l, RET_HEADS, RET_QK_DIM), pos, RET_QK_DIM, RET_THETA) * (RET_QK_DIM ** -0.5)
    rv = rv.reshape(b, l, RET_HEADS, RET_V_DIM)
    r_o, ret_state = retention(rq, rk, rv, ret_state0)
    r_o = r_o * lax.rsqrt(jnp.mean(r_o * r_o, axis=-1, keepdims=True) + NORM_EPS)
    r_out = (r_o.reshape(b, l, R_V_WIDTH) * jax.nn.silu(rg.astype(jnp.float32))).astype(x.dtype)
    x = x + jnp.concatenate([a_out, r_out], axis=-1) @ w_out
    h2 = rmsnorm(x, g_ffn)
    x = x + (jax.nn.silu(h2 @ w_gate) * (h2 @ w_up)) @ w_down
    return x, ak, av, ret_state.astype(x.dtype)


def setup_inputs(seed: int = 0) -> dict:
    key = jax.random.key(seed)
    ks = jax.random.split(key, 16)
    n_pages = PAST_LEN // PAGE_SIZE
    n_used = DEC_BATCH * n_pages
    n_pool = n_used + max(1, n_used // 4)
    f32 = jnp.float32
    x_prompt = jax.random.normal(ks[0], (BATCH, SEQ, D_MODEL), f32)
    x_sample = jax.random.normal(ks[1], (DEC_BATCH, DEC_SEQ, D_MODEL), f32)
    cache_k = jax.random.normal(ks[2], (DEPTH, n_pool, PAGE_SIZE, MOBA_HEADS, HEAD_DIM), f32)
    cache_v = jax.random.normal(ks[3], (DEPTH, n_pool, PAGE_SIZE, MOBA_HEADS, HEAD_DIM), f32)
    state_ret = jax.random.normal(ks[4], (DEPTH, DEC_BATCH, RET_HEADS, RET_QK_DIM, RET_V_DIM), f32)
    page_table = jax.random.permutation(ks[5], n_pool)[:n_used].reshape(DEC_BATCH, n_pages).astype(jnp.int32)
    w_in = jax.random.normal(ks[6], (DEPTH, D_MODEL, IN_COLS), f32) * D_MODEL ** -0.5
    w_out = jax.random.normal(ks[7], (DEPTH, MIX_WIDTH, D_MODEL), f32) * MIX_WIDTH ** -0.5
    g_mix = 1.0 + 0.01 * jax.random.normal(ks[8], (DEPTH, D_MODEL), f32)
    g_ffn = 1.0 + 0.01 * jax.random.normal(ks[9], (DEPTH, D_MODEL), f32)
    w_gate = jax.random.normal(ks[10], (DEPTH, D_MODEL, D_FF), f32) * D_MODEL ** -0.5
    w_up = jax.random.normal(ks[11], (DEPTH, D_MODEL, D_FF), f32) * D_MODEL ** -0.5
    w_down = jax.random.normal(ks[12], (DEPTH, D_FF, D_MODEL), f32) * D_FF ** -0.5
    g_final = 1.0 + 0.01 * jax.random.normal(ks[13], (D_MODEL,), f32)
    return {"x_prompt": x_prompt, "x_sample": x_sample, "cache_k": cache_k, "cache_v": cache_v,
            "state_ret": state_ret, "page_table": page_table, "w_in": w_in, "w_out": w_out,
            "g_mix": g_mix, "g_ffn": g_ffn, "w_gate": w_gate, "w_up": w_up, "w_down": w_down,
            "g_final": g_final}


def reference(x_prompt, x_sample, cache_k, cache_v, state_ret, page_table, w_in, w_out, g_mix, g_ffn,
              w_gate, w_up, w_down, g_final):
    past_len = page_table.shape[1] * cache_k.shape[2]
    xp = x_prompt
    xs = x_sample
    kp_l, vp_l, sp_l, ks_l, vs_l, ss_l = [], [], [], [], [], []
    for li in range(DEPTH):
        params = (w_in[li], w_out[li], g_mix[li], g_ffn[li], w_gate[li], w_up[li], w_down[li])
        zero_state = jnp.zeros((xp.shape[0], RET_HEADS, RET_QK_DIM, RET_V_DIM), xp.dtype)
        xp, kp, vp, sp = layer(xp, 0, moba_prompt, zero_state, *params)
        attend_s = functools.partial(moba_sample, cache_k[li], cache_v[li], page_table)
        xs, ks_, vs_, ss_ = layer(xs, past_len, attend_s, state_ret[li], *params)
        kp_l.append(kp); vp_l.append(vp); sp_l.append(sp)
        ks_l.append(ks_); vs_l.append(vs_); ss_l.append(ss_)
    y_prompt = rmsnorm(xp, g_final)
    y_sample = rmsnorm(xs, g_final)
    return (y_prompt, y_sample, jnp.stack(kp_l), jnp.stack(vp_l), jnp.stack(sp_l),
            jnp.stack(ks_l), jnp.stack(vs_l), jnp.stack(ss_l))
```

```python
import functools

import jax
import jax.numpy as jnp
from jax import lax
from jax.experimental import pallas as pl
from jax.experimental.pallas import tpu as pltpu

F32 = jnp.float32
BF16 = jnp.bfloat16
NEG_INF = float("-inf")

HEAD_DIM = 64
MOBA_BLOCK = 256
MOBA_TOPK = 3
ROPE_DIM = HEAD_DIM // 4
ROPE_THETA = 500000.0
MOBA_SCALE = HEAD_DIM ** -0.5
RET_HEADS = 4
RET_QK_DIM = 64
RET_V_DIM = 128
RET_THETA = 10000.0
RET_K_SCALE = RET_QK_DIM ** -0.5
NORM_EPS = 1e-6

LANES = 128
F32_SUBLANES = 8
BF16_SUBLANES = 16
VMEM_LIMIT_BYTES = 56 * 1024 * 1024

PROJ_ROWS = 512
RET_CHUNK = 256
RET_SAMPLE_PAD = BF16_SUBLANES
SAMPLE_PAGES_PER_STEP = 16
SAMPLE_NEW_PAD = LANES
FF_CHUNKS = 1

assert ROPE_DIM // 2 == F32_SUBLANES


def _dot(a, b):
    return jnp.dot(a, b, preferred_element_type=F32)


def _dot_nt(a, b):
    return lax.dot_general(a, b, (((1,), (1,)), ((), ())), preferred_element_type=F32)


def _dot_tn(a, b):
    return lax.dot_general(a, b, (((0,), (0,)), ((), ())), preferred_element_type=F32)


def _rmsnorm(x, g):
    return x * lax.rsqrt(jnp.mean(x * x, axis=-1, keepdims=True) + NORM_EPS) * g


def _rotary(x, cos, sin, half):
    lane = lax.broadcasted_iota(jnp.int32, x.shape, 1)
    first = (lane % HEAD_DIM) < half
    swap = jnp.where(first, pltpu.roll(x, LANES - half, 1), pltpu.roll(x, half, 1))
    return x * cos + swap * sin


def _rotary_wide(x, cos, sin, half):
    cols = [_rotary(x[:, c:c + LANES], cos, sin, half) for c in range(0, x.shape[1], LANES)]
    return jnp.concatenate(cols, axis=1)


def _rotary_t(xt, cos, sin):
    half = ROPE_DIM // 2
    pieces = []
    for r0 in range(0, xt.shape[0], HEAD_DIM):
        x1 = xt[r0:r0 + half]
        x2 = xt[r0 + half:r0 + 2 * half]
        pieces += [x1 * cos - x2 * sin, x2 * cos + x1 * sin, xt[r0 + 2 * half:r0 + HEAD_DIM]]
    return jnp.concatenate(pieces, axis=0)


def _topk_mask(gate, k, axis):
    idx = lax.broadcasted_iota(jnp.int32, gate.shape, axis).astype(F32)
    sel = jnp.zeros(gate.shape, jnp.bool_)
    for _ in range(k):
        top = jnp.max(gate, axis=axis, keepdims=True)
        first = jnp.min(jnp.where(gate == top, idx, float(gate.shape[axis])), axis=axis, keepdims=True)
        hit = idx == first
        sel = jnp.logical_or(sel, jnp.logical_and(hit, top > NEG_INF))
        gate = jnp.where(hit, NEG_INF, gate)
    return sel


def _rope_angles(pos, rot_dim, theta):
    half = rot_dim // 2
    inv_freq = jnp.power(theta, -jnp.arange(half, dtype=F32) * (2.0 / rot_dim))
    return pos.astype(F32)[:, None] * inv_freq[None, :]


def _rope_tables(pos, rot_dim, theta):
    ang = _rope_angles(pos, rot_dim, theta)
    cos, sin = jnp.cos(ang), jnp.sin(ang)
    n = pos.shape[0]
    rest = HEAD_DIM - rot_dim
    c_head = jnp.concatenate([cos, cos, jnp.ones((n, rest), F32)], axis=-1)
    s_head = jnp.concatenate([-sin, sin, jnp.zeros((n, rest), F32)], axis=-1)
    reps = LANES // HEAD_DIM
    return jnp.tile(c_head, (1, reps)), jnp.tile(s_head, (1, reps))


def _proj_prompt_kernel(x_ref, g_ref, wat_ref, wr_ref, cta_ref, sta_ref, cr_ref, sr_ref,
                        ktf_ref, vtf_ref, qt_ref, kb_ref, vt_ref, km_ref, rq_ref, rk_ref, rv_ref, rg_ref,
                        *, a_width, r_qk, r_v):
    h = _rmsnorm(x_ref[...], g_ref[...]).astype(BF16)
    cta, sta = cta_ref[...], sta_ref[...]
    qt = _rotary_t(_dot_nt(wat_ref[0:a_width, :], h), cta, sta) * MOBA_SCALE
    qt_ref[0] = qt.astype(BF16)
    kt = _rotary_t(_dot_nt(wat_ref[a_width:2 * a_width, :], h), cta, sta)
    ktf_ref[0] = kt
    vt = _dot_nt(wat_ref[2 * a_width:3 * a_width, :], h)
    vtf_ref[0] = vt
    k = kt.T
    kb_ref[0] = k.astype(BF16)
    for i in range(x_ref.shape[0] // MOBA_BLOCK):
        rows = slice(i * MOBA_BLOCK, (i + 1) * MOBA_BLOCK)
        vt_ref[0, i] = vt[:, rows].astype(BF16)
        km_ref[0, i:i + 1, :] = jnp.mean(k[rows], axis=0, keepdims=True)
    cr, sr = cr_ref[...], sr_ref[...]
    rq_ref[...] = _rotary_wide(_dot(h, wr_ref[:, 0:r_qk]), cr, sr, RET_QK_DIM // 2).astype(BF16)
    rk = _rotary_wide(_dot(h, wr_ref[:, r_qk:2 * r_qk]), cr, sr, RET_QK_DIM // 2)
    rk_ref[...] = (rk * RET_K_SCALE).astype(BF16)
    rv_ref[...] = _dot(h, wr_ref[:, 2 * r_qk:2 * r_qk + r_v]).astype(BF16)
    rg_ref[...] = _dot(h, wr_ref[:, 2 * r_qk + r_v:2 * r_qk + 2 * r_v])


def _proj_prompt(x2d, g, w_in_bf16, n_batch, seq):
    n, d = x2d.shape
    a_width = d // 2
    r_qk = RET_HEADS * RET_QK_DIM
    r_v = RET_HEADS * RET_V_DIM
    tm = PROJ_ROWS
    n_tiles = seq // tm
    nb = tm // MOBA_BLOCK
    n_blk = seq // MOBA_BLOCK
    pos = jnp.arange(seq, dtype=jnp.int32)
    ang = _rope_angles(pos, ROPE_DIM, ROPE_THETA)
    cta, sta = jnp.cos(ang).T, jnp.sin(ang).T
    cr, sr = _rope_tables(pos, RET_QK_DIM, RET_THETA)
    wat = w_in_bf16[:, :3 * a_width].T
    wr = w_in_bf16[:, 3 * a_width:]

    def row(j, b):
        return (b * n_tiles + j, 0)

    def const(j, b):
        return (0, 0)

    def feat(j, b):
        return (b, 0, j)

    in_specs = [
        pl.BlockSpec((tm, d), row),
        pl.BlockSpec((1, d), const),
        pl.BlockSpec(wat.shape, const, pipeline_mode=pl.Buffered(1)),
        pl.BlockSpec(wr.shape, const, pipeline_mode=pl.Buffered(1)),
        pl.BlockSpec((ROPE_DIM // 2, tm), lambda j, b: (0, j)),
        pl.BlockSpec((ROPE_DIM // 2, tm), lambda j, b: (0, j)),
        pl.BlockSpec((tm, LANES), lambda j, b: (j, 0)),
        pl.BlockSpec((tm, LANES), lambda j, b: (j, 0)),
    ]
    out_shape = [
        jax.ShapeDtypeStruct((n_batch, a_width, seq), F32),
        jax.ShapeDtypeStruct((n_batch, a_width, seq), F32),
        jax.ShapeDtypeStruct((n_batch, a_width, seq), BF16),
        jax.ShapeDtypeStruct((n_batch, seq, a_width), BF16),
        jax.ShapeDtypeStruct((n_batch, n_blk, a_width, MOBA_BLOCK), BF16),
        jax.ShapeDtypeStruct((n_batch * n_tiles, nb, a_width), F32),
        jax.ShapeDtypeStruct((n, r_qk), BF16), jax.ShapeDtypeStruct((n, r_qk), BF16),
        jax.ShapeDtypeStruct((n, r_v), BF16), jax.ShapeDtypeStruct((n, r_v), F32),
    ]
    out_specs = [
        pl.BlockSpec((1, a_width, tm), feat),
        pl.BlockSpec((1, a_width, tm), feat),
        pl.BlockSpec((1, a_width, tm), feat),
        pl.BlockSpec((1, tm, a_width), lambda j, b: (b, j, 0)),
        pl.BlockSpec((1, nb, a_width, MOBA_BLOCK), lambda j, b: (b, j, 0, 0)),
        pl.BlockSpec((1, nb, a_width), lambda j, b: (b * n_tiles + j, 0, 0)),
        pl.BlockSpec((tm, r_qk), row), pl.BlockSpec((tm, r_qk), row),
        pl.BlockSpec((tm, r_v), row), pl.BlockSpec((tm, r_v), row),
    ]
    return pl.pallas_call(
        functools.partial(_proj_prompt_kernel, a_width=a_width, r_qk=r_qk, r_v=r_v),
        grid=(n_tiles, n_batch),
        in_specs=in_specs, out_specs=out_specs, out_shape=out_shape,
        compiler_params=pltpu.CompilerParams(vmem_limit_bytes=VMEM_LIMIT_BYTES),
        name="proj_prompt",
    )(x2d, g.reshape(1, d), wat, wr, cta, sta, cr, sr)


def _proj_sample_kernel(x_ref, g_ref, w_ref, ca_ref, sa_ref, cr_ref, sr_ref,
                        kf_ref, vf_ref, q_ref, rq_ref, rk_ref, rv_ref, rg_ref, *, a_width, r_qk, r_v):
    h = _rmsnorm(x_ref[...], g_ref[...]).astype(BF16)
    offs = [0]
    for wdt in (a_width, a_width, a_width, r_qk, r_qk, r_v, r_v):
        offs.append(offs[-1] + wdt)

    def proj(i):
        return _dot(h, w_ref[:, offs[i]:offs[i + 1]])

    ca, sa, cr, sr = ca_ref[...], sa_ref[...], cr_ref[...], sr_ref[...]
    q_ref[...] = _rotary_wide(proj(0), ca, sa, ROPE_DIM // 2) * MOBA_SCALE
    kf_ref[...] = _rotary_wide(proj(1), ca, sa, ROPE_DIM // 2)
    vf_ref[...] = proj(2)
    rq_ref[...] = _rotary_wide(proj(3), cr, sr, RET_QK_DIM // 2).astype(BF16)
    rk_ref[...] = (_rotary_wide(proj(4), cr, sr, RET_QK_DIM // 2) * RET_K_SCALE).astype(BF16)
    rv_ref[...] = proj(5).astype(BF16)
    rg_ref[...] = proj(6)


def _proj_sample(x2d, g, w_in_bf16, pos):
    n, d = x2d.shape
    a_width = d // 2
    r_qk = RET_HEADS * RET_QK_DIM
    r_v = RET_HEADS * RET_V_DIM
    tm = pos.shape[0]
    ca, sa = _rope_tables(pos, ROPE_DIM, ROPE_THETA)
    cr, sr = _rope_tables(pos, RET_QK_DIM, RET_THETA)

    def row(i):
        return (i, 0)

    def const(i):
        return (0, 0)

    widths_dtypes = [(a_width, F32), (a_width, F32), (a_width, F32), (r_qk, BF16), (r_qk, BF16), (r_v, BF16),
                     (r_v, F32)]
    return pl.pallas_call(
        functools.partial(_proj_sample_kernel, a_width=a_width, r_qk=r_qk, r_v=r_v),
        grid=(n // tm,),
        in_specs=[
            pl.BlockSpec((tm, d), row),
            pl.BlockSpec((1, d), const),
            pl.BlockSpec(w_in_bf16.shape, const, pipeline_mode=pl.Buffered(1)),
            pl.BlockSpec((tm, LANES), const), pl.BlockSpec((tm, LANES), const),
            pl.BlockSpec((tm, LANES), const), pl.BlockSpec((tm, LANES), const),
        ],
        out_specs=[pl.BlockSpec((tm, w), row) for w, _ in widths_dtypes],
        out_shape=[jax.ShapeDtypeStruct((n, w), dt) for w, dt in widths_dtypes],
        compiler_params=pltpu.CompilerParams(vmem_limit_bytes=VMEM_LIMIT_BYTES),
        name="proj_sample",
    )(x2d, g.reshape(1, d), w_in_bf16, ca, sa, cr, sr)


def _moba_prompt_kernel(qt_ref, kb_ref, vt_ref, km_ref, o_ref, sel_ref):
    cur = pl.program_id(1)
    n_blk = km_ref.shape[1]
    a_width = km_ref.shape[2]
    blk = qt_ref.shape[2]
    key_i = lax.broadcasted_iota(jnp.int32, (blk, blk), 0)
    qry_i = lax.broadcasted_iota(jnp.int32, (blk, blk), 1)
    blk_i = lax.broadcasted_iota(jnp.int32, (n_blk, blk), 0)
    zeros_half = jnp.zeros((HEAD_DIM, blk), BF16)

    def k_block(j, cols):
        return kb_ref[0, pl.ds(pl.multiple_of(j * blk, blk), blk), cols]

    for c in range(0, a_width, LANES):
        cols = slice(c, c + LANES)
        kmp = km_ref[0, :, cols].astype(BF16)
        qtp = qt_ref[0, cols, :]
        halves = []
        for hh in range(LANES // HEAD_DIM):
            if hh == 0:
                qtm = jnp.concatenate([qtp[:HEAD_DIM], zeros_half], axis=0)
            else:
                qtm = jnp.concatenate([zeros_half, qtp[HEAD_DIM:]], axis=0)
            gate = jnp.where(blk_i < cur, _dot(kmp, qtm), NEG_INF)
            sel_ref[...] = _topk_mask(gate, MOBA_TOPK, 0).astype(F32)
            s = _dot(k_block(cur, cols), qtm)
            s = jnp.where(key_i <= qry_i, s, NEG_INF)
            m = jnp.max(s, axis=0, keepdims=True)
            p = jnp.exp(s - m)
            l = jnp.sum(p, axis=0, keepdims=True)
            acc = _dot(vt_ref[0, cur, cols, :], p.astype(BF16))

            def body(j, carry, cols=cols, qtm=qtm):
                m, l, acc = carry
                s = _dot(k_block(j, cols), qtm)
                s = jnp.where(sel_ref[pl.ds(j, 1), :] > 0.0, s, NEG_INF)
                m_new = jnp.maximum(m, jnp.max(s, axis=0, keepdims=True))
                alpha = jnp.exp(m - m_new)
                p = jnp.exp(s - m_new)
                l = alpha * l + jnp.sum(p, axis=0, keepdims=True)
                acc = alpha * acc + _dot(vt_ref[0, j, cols, :], p.astype(BF16))
                return m_new, l, acc

            m, l, acc = lax.fori_loop(0, cur, body, (m, l, acc))
            o_t = acc / l
            halves.append(o_t[hh * HEAD_DIM:(hh + 1) * HEAD_DIM])
        o_ref[0, :, cols] = jnp.concatenate(halves, axis=0).T.astype(BF16)


def _moba_prompt(qt, kb, vt, km):
    n_batch, n_blk, a_width, blk = vt.shape
    return pl.pallas_call(
        _moba_prompt_kernel,
        grid=(n_batch, n_blk),
        in_specs=[
            pl.BlockSpec((1, a_width, blk), lambda b, i: (b, 0, i)),
            pl.BlockSpec((1, n_blk * blk, a_width), lambda b, i: (b, 0, 0)),
            pl.BlockSpec((1, n_blk, a_width, blk), lambda b, i: (b, 0, 0, 0)),
            pl.BlockSpec((1, n_blk, a_width), lambda b, i: (b, 0, 0)),
        ],
        out_specs=pl.BlockSpec((1, blk, a_width), lambda b, i: (b, i, 0)),
        out_shape=jax.ShapeDtypeStruct((n_batch, n_blk * blk, a_width), BF16),
        scratch_shapes=[pltpu.VMEM((n_blk, blk), F32)],
        compiler_params=pltpu.CompilerParams(vmem_limit_bytes=VMEM_LIMIT_BYTES),
        name="moba_prompt",
    )(qt, kb, vt, km)


def _moba_sample_kernel(pt_ref, qb_ref, kn_ref, vn_ref, *refs, n_pages_step, pages_per_blk, dec_seq):
    del pt_ref
    kp = refs[:n_pages_step]
    vp = refs[n_pages_step:2 * n_pages_step]
    o_ref = refs[2 * n_pages_step]
    obuf, mmat, lmat, gmat = refs[2 * n_pages_step + 1:]
    g = pl.program_id(1)
    n_rows, a_width = qb_ref.shape[1], qb_ref.shape[2]
    n_heads = a_width // HEAD_DIM
    blk_step = n_pages_step // pages_per_blk
    n_blk = obuf.shape[0]
    blk_keys = pages_per_blk * kp[0].shape[2]

    @pl.when(g == 0)
    def _init():
        mmat[...] = jnp.full(mmat.shape, NEG_INF, F32)
        lmat[...] = jnp.zeros(lmat.shape, F32)
        gmat[...] = jnp.full(gmat.shape, NEG_INF, F32)

    qb = qb_ref[0]
    row_w = lax.broadcasted_iota(jnp.int32, (n_rows, a_width), 0)
    lane_w = lax.broadcasted_iota(jnp.int32, (n_rows, a_width), 1)
    head_mask = (lane_w // HEAD_DIM == row_w % n_heads).astype(F32)
    lane_b = lax.broadcasted_iota(jnp.int32, (n_rows, LANES), 1)
    for jb in range(blk_step):
        j = g * blk_step + jb
        pages = range(jb * pages_per_blk, (jb + 1) * pages_per_blk)
        kt = jnp.concatenate([kp[i][0] for i in pages], axis=1)
        vt = jnp.concatenate([vp[i][0] for i in pages], axis=1)
        s = _dot(qb, kt.astype(BF16))
        gate = jnp.sum(s, axis=-1, keepdims=True) * (1.0 / blk_keys)
        m = jnp.max(s, axis=-1, keepdims=True)
        p = jnp.exp(s - m)
        l = jnp.sum(p, axis=-1, keepdims=True)
        obuf[j] = _dot_nt(p.astype(BF16), vt.astype(BF16)) * head_mask
        mmat[...] = jnp.where(lane_b == j, m, mmat[...])
        lmat[...] = jnp.where(lane_b == j, l, lmat[...])
        gmat[...] = jnp.where(lane_b == j, gate, gmat[...])

    @pl.when(g == pl.num_programs(1) - 1)
    def _finalize():
        sel = _topk_mask(gmat[...], MOBA_TOPK, 1)
        s_o = _dot_nt(qb, kn_ref[0])
        t_row = lax.broadcasted_iota(jnp.int32, s_o.shape, 0) // n_heads
        key = lax.broadcasted_iota(jnp.int32, s_o.shape, 1)
        s_o = jnp.where(jnp.logical_and(key <= t_row, key < dec_seq), s_o, NEG_INF)
        m_o = jnp.max(s_o, axis=-1, keepdims=True)
        p_o = jnp.exp(s_o - m_o)
        l_o = jnp.sum(p_o, axis=-1, keepdims=True)
        o_o = _dot(p_o.astype(BF16), vn_ref[0]) * head_mask
        mm = mmat[...]
        m_all = jnp.maximum(jnp.max(jnp.where(sel, mm, NEG_INF), axis=-1, keepdims=True), m_o)
        w = jnp.where(sel, jnp.exp(mm - m_all), 0.0)
        w_o = jnp.exp(m_o - m_all)
        l_all = jnp.sum(w * lmat[...], axis=-1, keepdims=True) + w_o * l_o
        o = w_o * o_o
        for jj in range(n_blk):
            o = o + w[:, jj:jj + 1] * obuf[jj]
        o = o / l_all
        o_ref[0] = jnp.sum(o.reshape(n_rows // n_heads, n_heads, a_width), axis=1)


def _moba_sample(cache_kt, cache_vt, page_table, q, k_new, v_new):
    n_seq, n_pages = page_table.shape
    _, a_width, page = cache_kt.shape
    dec_seq = q.shape[1]
    n_heads = a_width // HEAD_DIM
    pages_per_blk = MOBA_BLOCK // page
    n_blk = n_pages // pages_per_blk
    n_step = SAMPLE_PAGES_PER_STEP
    assert MOBA_BLOCK % page == 0 and n_pages % n_step == 0 and n_step % pages_per_blk == 0
    assert (n_pages * page) % MOBA_BLOCK == 0 and dec_seq <= MOBA_BLOCK and n_blk <= LANES
    n_rows = dec_seq * n_heads
    head_of_lane = jnp.arange(a_width, dtype=jnp.int32) // HEAD_DIM
    qb = jnp.repeat(q, n_heads, axis=1)
    qb = jnp.where(head_of_lane[None, None, :] == (jnp.arange(n_rows, dtype=jnp.int32) % n_heads)[None, :, None],
                   qb, 0.0).astype(BF16)
    pad = ((0, 0), (0, SAMPLE_NEW_PAD - dec_seq), (0, 0))
    kn = jnp.pad(k_new, pad).astype(BF16)
    vn = jnp.pad(v_new, pad).astype(BF16)

    def seq3(b, g, pt):
        return (b, 0, 0)

    def page_spec(i):
        return pl.BlockSpec((1, a_width, page), lambda b, g, pt: (pt[b, g * n_step + i], 0, 0))

    grid_spec = pltpu.PrefetchScalarGridSpec(
        num_scalar_prefetch=1,
        grid=(n_seq, n_pages // n_step),
        in_specs=[
            pl.BlockSpec((1, n_rows, a_width), seq3),
            pl.BlockSpec((1, SAMPLE_NEW_PAD, a_width), seq3),
            pl.BlockSpec((1, SAMPLE_NEW_PAD, a_width), seq3),
        ] + [page_spec(i) for i in range(n_step)] + [page_spec(i) for i in range(n_step)],
        out_specs=pl.BlockSpec((1, dec_seq, a_width), seq3),
        scratch_shapes=[
            pltpu.VMEM((n_blk, n_rows, a_width), F32),
            pltpu.VMEM((n_rows, LANES), F32),
            pltpu.VMEM((n_rows, LANES), F32),
            pltpu.VMEM((n_rows, LANES), F32),
        ],
    )
    return pl.pallas_call(
        functools.partial(_moba_sample_kernel, n_pages_step=n_step, pages_per_blk=pages_per_blk, dec_seq=dec_seq),
        grid_spec=grid_spec,
        out_shape=jax.ShapeDtypeStruct((n_seq, dec_seq, a_width), F32),
        compiler_params=pltpu.CompilerParams(vmem_limit_bytes=VMEM_LIMIT_BYTES),
        name="moba_sample",
    )(page_table, qb, kn, vn, *([cache_kt] * n_step), *([cache_vt] * n_step))


def _retention_kernel(*refs, has_state):
    if has_state:
        rq_ref, rk_ref, rv_ref, rg_ref, dm_ref, qd_ref, kd_ref, cd_ref, s0_ref, o_ref, sf_ref, s_ref = refs
    else:
        rq_ref, rk_ref, rv_ref, rg_ref, dm_ref, qd_ref, kd_ref, cd_ref, o_ref, sf_ref, s_ref = refs
    ci = pl.program_id(1)

    @pl.when(ci == 0)
    def _init():
        if has_state:
            s_ref[...] = s0_ref[0]
        else:
            s_ref[...] = jnp.zeros(s_ref.shape, F32)

    heads_per_tile = LANES // RET_QK_DIM
    for pr in range(RET_HEADS // heads_per_tile):
        qk_cols = slice(pr * LANES, (pr + 1) * LANES)
        qf = rq_ref[:, qk_cols].astype(F32)
        kf = rk_ref[:, qk_cols].astype(F32)
        kp = rk_ref[:, qk_cols]
        s_pair = s_ref[pr]
        s_new = s_pair * cd_ref[pr]
        lane = lax.broadcasted_iota(jnp.int32, qf.shape, 1)
        for hh in range(heads_per_tile):
            h = pr * heads_per_tile + hh
            mine = lane // RET_QK_DIM == hh
            v_cols = slice(h * RET_V_DIM, (h + 1) * RET_V_DIM)
            vh = rv_ref[:, v_cols]
            qm = jnp.where(mine, qf, 0.0)
            inner = _dot_nt(qm.astype(BF16), kp) * dm_ref[h]
            o = _dot(inner.astype(BF16), vh)
            o = o + _dot((qm * qd_ref[h]).astype(BF16), s_pair.astype(BF16))
            km = jnp.where(mine, kf, 0.0) * kd_ref[h]
            s_new = s_new + _dot_tn(km.astype(BF16), vh)
            o = o * lax.rsqrt(jnp.mean(o * o, axis=-1, keepdims=True) + NORM_EPS)
            gate = rg_ref[:, v_cols]
            o_ref[:, v_cols] = (o * (gate / (1.0 + jnp.exp(-gate)))).astype(BF16)
        s_ref[pr] = s_new

    @pl.when(ci == pl.num_programs(1) - 1)
    def _done():
        sf_ref[0] = s_ref[...]


def _retention_tables(c, length):
    log_g = jnp.log1p(-jnp.power(2.0, -5.0 - jnp.arange(RET_HEADS, dtype=F32)))
    idx = jnp.arange(c, dtype=F32)
    diff = idx[:, None] - idx[None, :]
    causal = diff >= 0
    decay_mask = jnp.exp(jnp.where(causal, diff, 0.0)[None] * log_g[:, None, None]) * causal[None]
    q_decay = jnp.exp((idx + 1.0)[None, :] * log_g[:, None])
    k_decay = jnp.exp((length - 1.0 - idx)[None, :] * log_g[:, None])
    chunk_decay = jnp.exp(length * log_g)
    heads_per_tile = LANES // RET_QK_DIM
    qd = jnp.broadcast_to(q_decay[:, :, None], (RET_HEADS, c, LANES))
    kd = jnp.broadcast_to(k_decay[:, :, None], (RET_HEADS, c, LANES))
    cd = jnp.repeat(chunk_decay, RET_QK_DIM).reshape(RET_HEADS // heads_per_tile, LANES, 1)
    cd = jnp.broadcast_to(cd, (RET_HEADS // heads_per_tile, LANES, RET_V_DIM))
    return decay_mask, qd, kd, cd


def _retention(rq, rk, rv, rg, n_batch, c, length, state0):
    n, r_qk = rq.shape
    r_v = rv.shape[1]
    n_chunks = n // (n_batch * c)
    n_pairs = RET_HEADS * RET_QK_DIM // LANES
    dm, qd, kd, cd = _retention_tables(c, length)

    def row(b, i):
        return (b * n_chunks + i, 0)

    def const3(b, i):
        return (0, 0, 0)

    def state(b, i):
        return (b, 0, 0, 0)

    in_specs = [
        pl.BlockSpec((c, r_qk), row), pl.BlockSpec((c, r_qk), row),
        pl.BlockSpec((c, r_v), row), pl.BlockSpec((c, r_v), row),
        pl.BlockSpec(dm.shape, const3), pl.BlockSpec(qd.shape, const3),
        pl.BlockSpec(kd.shape, const3), pl.BlockSpec(cd.shape, const3),
    ]
    args = [rq, rk, rv, rg, dm, qd, kd, cd]
    state_shape = (n_batch, n_pairs, LANES, RET_V_DIM)
    if state0 is not None:
        in_specs.append(pl.BlockSpec((1, n_pairs, LANES, RET_V_DIM), state))
        args.append(state0.reshape(state_shape))
    o, s_fin = pl.pallas_call(
        functools.partial(_retention_kernel, has_state=state0 is not None),
        grid=(n_batch, n_chunks),
        in_specs=in_specs,
        out_specs=[pl.BlockSpec((c, r_v), row), pl.BlockSpec((1, n_pairs, LANES, RET_V_DIM), state)],
        out_shape=[jax.ShapeDtypeStruct((n, r_v), BF16), jax.ShapeDtypeStruct(state_shape, F32)],
        scratch_shapes=[pltpu.VMEM((n_pairs, LANES, RET_V_DIM), F32)],
        compiler_params=pltpu.CompilerParams(vmem_limit_bytes=VMEM_LIMIT_BYTES),
        name="retention_sample" if state0 is not None else "retention_prompt",
    )(*args)
    return o, s_fin.reshape(n_batch, RET_HEADS, RET_QK_DIM, RET_V_DIM)


def _mlp_kernel(x_ref, a_ref, r_ref, wo_ref, g2_ref, wg_ref, wu_ref, wd_ref, gf_ref, y_ref, *, final):
    a_width = a_ref.shape[1]
    x1 = x_ref[...] + _dot(a_ref[...], wo_ref[:a_width, :]) + _dot(r_ref[...], wo_ref[a_width:, :])
    h2 = _rmsnorm(x1, g2_ref[...]).astype(BF16)
    d_ff = wg_ref.shape[1]
    step = d_ff // FF_CHUNKS
    acc = x1
    for c in range(0, d_ff, step):
        gate = _dot(h2, wg_ref[:, c:c + step])
        up = _dot(h2, wu_ref[:, c:c + step])
        act = (gate / (1.0 + jnp.exp(-gate)) * up).astype(BF16)
        acc = acc + _dot(act, wd_ref[c:c + step, :])
    y_ref[...] = _rmsnorm(acc, gf_ref[...]) if final else acc


def _mlp(x2d, a_out, r_out, wo, g2, wg, wu, wd, gf, *, final):
    n, d = x2d.shape
    tm = min(PROJ_ROWS, n)
    assert wg.shape[1] % (FF_CHUNKS * LANES) == 0

    def row(i):
        return (i, 0)

    def const(i):
        return (0, 0)

    def resident(shape):
        return pl.BlockSpec(shape, const, pipeline_mode=pl.Buffered(1))

    return pl.pallas_call(
        functools.partial(_mlp_kernel, final=final),
        grid=(n // tm,),
        in_specs=[
            pl.BlockSpec((tm, d), row),
            pl.BlockSpec((tm, a_out.shape[1]), row),
            pl.BlockSpec((tm, r_out.shape[1]), row),
            resident(wo.shape), pl.BlockSpec((1, d), const),
            resident(wg.shape), resident(wu.shape), resident(wd.shape),
            pl.BlockSpec((1, d), const),
        ],
        out_specs=pl.BlockSpec((tm, d), row),
        out_shape=jax.ShapeDtypeStruct((n, d), F32),
        compiler_params=pltpu.CompilerParams(vmem_limit_bytes=VMEM_LIMIT_BYTES),
        name="mlp",
    )(x2d, a_out, r_out, wo, g2.reshape(1, d), wg, wu, wd, gf.reshape(1, d))


def kernel(x_prompt, x_sample, cache_k, cache_v, state_ret, page_table, w_in, w_out, g_mix, g_ffn, w_gate, w_up,
           w_down, g_final):
    n_b, seq, d = x_prompt.shape
    n_s, dec_seq, _ = x_sample.shape
    depth = w_in.shape[0]
    a_width = d // 2
    n_heads = a_width // HEAD_DIM
    page = cache_k.shape[2]
    past_len = page_table.shape[1] * page
    assert seq % PROJ_ROWS == 0 and PROJ_ROWS % MOBA_BLOCK == 0 and seq % RET_CHUNK == 0
    assert (n_s * dec_seq) % PROJ_ROWS == 0 or n_s * dec_seq < PROJ_ROWS
    assert dec_seq <= RET_SAMPLE_PAD

    xp = x_prompt.reshape(n_b * seq, d)
    xs = x_sample.reshape(n_s * dec_seq, d)
    tm_s = min(PROJ_ROWS, n_s * dec_seq)
    assert tm_s % dec_seq == 0
    pos_s = past_len + jnp.arange(tm_s, dtype=jnp.int32) % dec_seq
    outs = [[] for _ in range(6)]
    for li in range(depth):
        final = li == depth - 1
        w_in_b = w_in[li].astype(BF16)
        wo_b, wg_b, wu_b, wd_b = (w[li].astype(BF16) for w in (w_out, w_gate, w_up, w_down))

        ktf, vtf, qt, kb, vt, km, rq, rk, rv, rg = _proj_prompt(xp, g_mix[li], w_in_b, n_b, seq)
        a_out = _moba_prompt(qt, kb, vt, km.reshape(n_b, seq // MOBA_BLOCK, a_width))
        r_out, s_p = _retention(rq, rk, rv, rg, n_b, RET_CHUNK, RET_CHUNK, None)
        xp = _mlp(xp, a_out.reshape(n_b * seq, a_width), r_out, wo_b, g_ffn[li], wg_b, wu_b, wd_b, g_final,
                  final=final)
        outs[0].append(ktf.reshape(n_b, n_heads, HEAD_DIM, seq).transpose(0, 3, 1, 2))
        outs[1].append(vtf.reshape(n_b, n_heads, HEAD_DIM, seq).transpose(0, 3, 1, 2))
        outs[2].append(s_p)

        kf, vf, q, rq, rk, rv, rg = _proj_sample(xs, g_mix[li], w_in_b, pos_s)
        cache_kt = cache_k[li].transpose(0, 2, 3, 1).reshape(-1, a_width, page)
        cache_vt = cache_v[li].transpose(0, 2, 3, 1).reshape(-1, a_width, page)
        a_out = _moba_sample(
            cache_kt, cache_vt, page_table,
            q.reshape(n_s, dec_seq, a_width), kf.reshape(n_s, dec_seq, a_width), vf.reshape(n_s, dec_seq, a_width))

        def pad_tokens(t):
            t = t.reshape(n_s, dec_seq, t.shape[-1])
            t = jnp.pad(t, ((0, 0), (0, RET_SAMPLE_PAD - dec_seq), (0, 0)))
            return t.reshape(n_s * RET_SAMPLE_PAD, t.shape[-1])

        r_out, s_s = _retention(pad_tokens(rq), pad_tokens(rk), pad_tokens(rv), pad_tokens(rg), n_s,
                                RET_SAMPLE_PAD, dec_seq, state_ret[li])
        r_out = r_out.reshape(n_s, RET_SAMPLE_PAD, -1)[:, :dec_seq].reshape(n_s * dec_seq, -1)
        xs = _mlp(xs, a_out.reshape(n_s * dec_seq, a_width).astype(BF16), r_out, wo_b, g_ffn[li], wg_b, wu_b, wd_b,
                  g_final, final=final)
        outs[3].append(kf.reshape(n_s, dec_seq, n_heads, HEAD_DIM))
        outs[4].append(vf.reshape(n_s, dec_seq, n_heads, HEAD_DIM))
        outs[5].append(s_s)

    kp, vp, sp, ks, vs, ss = (jnp.stack(o) for o in outs)
    return (xp.reshape(n_b, seq, d), xs.reshape(n_s, dec_seq, d), kp, vp, sp, ks, vs, ss)
```
